```python
import math
import jax, jax.numpy as jnp
from jax import lax
import numpy as np

D_MODEL = 4096
BATCH = 1
SEQ = 8192
DEPTH = 1

CHUNK = 64
W_A = 4096
H_A = 16
BW_A = W_A // H_A
CONV_W = 4
LRU_C = 8.0
W_B = 4096
G_B = 16
DG_B = W_B // G_B
SPATIAL = 128
MIX = W_A + W_B
IN_COLS = 2 * W_A + 3 * W_B
EPS = 1e-6

kernel_name = "hybrid_rglru_gmlp_parallel_heads"


def rmsnorm(x, g):
    xf = x.astype(jnp.float32)
    y = xf * lax.rsqrt(jnp.mean(xf * xf, axis=-1, keepdims=True) + EPS)
    return (y * g.astype(jnp.float32)).astype(x.dtype)


def layernorm_f32(x, g, b):
    xf = x.astype(jnp.float32)
    mu = jnp.mean(xf, axis=-1, keepdims=True)
    var = jnp.mean(jnp.square(xf - mu), axis=-1, keepdims=True)
    return (xf - mu) * lax.rsqrt(var + EPS) * g.astype(jnp.float32) + b.astype(jnp.float32)


def causal_depthwise_conv(x, w, b):
    S = x.shape[1]
    xp = jnp.pad(x, ((0, 0), (CONV_W - 1, 0), (0, 0)))
    y = b
    for k in range(CONV_W):
        y = y + xp[:, k:k + S, :] * w[k]
    return y


def rg_lru(x, w_a, b_a, w_x, b_x, lam):
    B, S, _ = x.shape
    xh = x.reshape(B, S, H_A, BW_A)
    r = jax.nn.sigmoid(jnp.einsum('bshi,hij->bshj', xh, w_a) + b_a).reshape(B, S, W_A)
    i = jax.nn.sigmoid(jnp.einsum('bshi,hij->bshj', xh, w_x) + b_x).reshape(B, S, W_A)
    log_a = -LRU_C * r.astype(jnp.float32) * jax.nn.softplus(-lam.astype(jnp.float32))
    a = jnp.exp(log_a)
    inp = jnp.sqrt(-jnp.expm1(2.0 * log_a)) * (i.astype(jnp.float32) * x.astype(jnp.float32))

    def combine(left, right):
        a_l, b_l = left
        a_r, b_r = right
        return a_l * a_r, a_r * b_l + b_r

    _, h = lax.associative_scan(combine, (a, inp), axis=1)
    return h.astype(x.dtype)


def spatial_gating(u, v, ln_g, ln_b, w_sp, b_sp):
    B, S, _ = v.shape
    n = S // SPATIAL
    vn = layernorm_f32(v, ln_g, ln_b).reshape(B, n, SPATIAL, G_B, DG_B)
    blk = jnp.arange(SPATIAL) // CHUNK
    mask = (blk[None, :] <= blk[:, None]).astype(jnp.float32)
    ws = w_sp.astype(jnp.float32) * mask
    s = jnp.einsum('gij,bnjgd->bnigd', ws, vn) + b_sp.astype(jnp.float32).T[None, None, :, :, None]
    return (u.astype(jnp.float32) * s.reshape(B, S, W_B)).astype(u.dtype)


def setup_inputs(seed: int = 0) -> dict:
    key = jax.random.key(seed)
    ks = jax.random.split(key, 16)
    f32 = jnp.float32
    x = jax.random.normal(ks[0], (BATCH, SEQ, D_MODEL), f32)
    norm_g = 1.0 + 0.1 * jax.random.normal(ks[1], (DEPTH, D_MODEL), f32)
    w_in = jax.random.normal(ks[2], (DEPTH, D_MODEL, IN_COLS), f32) * D_MODEL ** -0.5
    conv_w = jax.random.normal(ks[3], (DEPTH, CONV_W, W_A), f32) * CONV_W ** -0.5
    conv_b = 0.01 * jax.random.normal(ks[4], (DEPTH, W_A), f32)
    w_gate_a = jax.random.normal(ks[5], (DEPTH, H_A, BW_A, BW_A), f32) * BW_A ** -0.5
    b_gate_a = 0.01 * jax.random.normal(ks[6], (DEPTH, H_A, BW_A), f32)
    w_gate_x = jax.random.normal(ks[7], (DEPTH, H_A, BW_A, BW_A), f32) * BW_A ** -0.5
    b_gate_x = 0.01 * jax.random.normal(ks[8], (DEPTH, H_A, BW_A), f32)
    u0 = jax.random.uniform(ks[9], (DEPTH, W_A), f32, minval=0.9, maxval=0.999)
    a0 = u0 ** (1.0 / LRU_C)
    lru_lambda = jnp.log(a0) - jnp.log1p(-a0)
    ln_v_g = 1.0 + 0.1 * jax.random.normal(ks[10], (DEPTH, W_B), f32)
    ln_v_b = 0.01 * jax.random.normal(ks[11], (DEPTH, W_B), f32)
    w_spatial = 0.5 * jax.random.normal(ks[12], (DEPTH, G_B, SPATIAL, SPATIAL), f32) * SPATIAL ** -0.5
    b_spatial = 1.0 + 0.1 * jax.random.normal(ks[13], (DEPTH, G_B, SPATIAL), f32)
    w_out = jax.random.normal(ks[14], (DEPTH, MIX, D_MODEL), f32) * MIX ** -0.5
    final_g = 1.0 + 0.1 * jax.random.normal(ks[15], (D_MODEL,), f32)
    return {"x": x, "norm_g": norm_g, "w_in": w_in, "conv_w": conv_w, "conv_b": conv_b,
            "w_gate_a": w_gate_a, "b_gate_a": b_gate_a, "w_gate_x": w_gate_x, "b_gate_x": b_gate_x,
            "lru_lambda": lru_lambda, "ln_v_g": ln_v_g, "ln_v_b": ln_v_b,
            "w_spatial": w_spatial, "b_spatial": b_spatial, "w_out": w_out, "final_g": final_g}


def reference(x, norm_g, w_in, conv_w, conv_b, w_gate_a, b_gate_a, w_gate_x, b_gate_x,
              lru_lambda, ln_v_g, ln_v_b, w_spatial, b_spatial, w_out, final_g):
    for l in range(DEPTH):
        hn = rmsnorm(x, norm_g[l])
        proj = jnp.einsum('bsd,de->bse', hn, w_in[l])
        xa, ga, u, v, gb = jnp.split(proj, [W_A, 2 * W_A, 2 * W_A + W_B, 2 * W_A + 2 * W_B], axis=-1)
        xa = causal_depthwise_conv(xa, conv_w[l], conv_b[l])
        ya = rg_lru(xa, w_gate_a[l], b_gate_a[l], w_gate_x[l], b_gate_x[l], lru_lambda[l])
        yb = spatial_gating(jax.nn.gelu(u, approximate=False), jax.nn.gelu(v, approximate=False),
                            ln_v_g[l], ln_v_b[l], w_spatial[l], b_spatial[l])
        mixed = jnp.concatenate([ya * jax.nn.silu(ga), yb * jax.nn.silu(gb)], axis=-1)
        x = x + jnp.einsum('bse,ed->bsd', mixed, w_out[l])
    return rmsnorm(x, final_g)
```

```python
import functools
import math

import jax
import jax.numpy as jnp
from jax import lax
from jax.experimental import pallas as pl
from jax.experimental.pallas import tpu as pltpu

H = 16
BW = 256
CONV_W = 4
LRU_C = 8.0
SPATIAL = 128
CHUNK = 64
EPS = 1e-6

ROW_TILE = 512
V_COLS = 512
OUT_K = 512
SUBLANES = 8
VMEM_LIMIT = 56 * 1024 * 1024

F32 = jnp.float32
BF16 = jnp.bfloat16


def _gelu(x):
    return 0.5 * x * (1.0 + lax.erf(x * math.sqrt(0.5)))


def _sigmoid(x):
    return 1.0 / (1.0 + jnp.exp(-x))


def _dot(a, b):
    return jnp.dot(a, b, preferred_element_type=F32)


def _vproj_kernel(x_ref, g_ref, wv_ref, hn_ref, gv_ref, mu_ref, rstd_ref, s1_ref, s2_ref,
                  *, n_col_steps, width):
    j = pl.program_id(1)

    @pl.when(j == 0)
    def _():
        x = x_ref[...]
        ms = jnp.mean(x * x, axis=-1, keepdims=True)
        hn_ref[...] = (x * lax.rsqrt(ms + EPS) * g_ref[...]).astype(BF16)
        s1_ref[...] = jnp.zeros_like(s1_ref)
        s2_ref[...] = jnp.zeros_like(s2_ref)

    gv = _gelu(_dot(hn_ref[...], wv_ref[...]))
    gv_ref[...] = gv.astype(BF16)
    s1_ref[...] += jnp.sum(gv, axis=-1, keepdims=True)
    s2_ref[...] += jnp.sum(gv * gv, axis=-1, keepdims=True)

    @pl.when(j == n_col_steps - 1)
    def _():
        mu = s1_ref[...] * (1.0 / width)
        var = jnp.maximum(s2_ref[...] * (1.0 / width) - mu * mu, 0.0)
        mu_ref[...] = jnp.broadcast_to(mu, mu_ref.shape)
        rstd_ref[...] = jnp.broadcast_to(lax.rsqrt(var + EPS), rstd_ref.shape)


def _vproj(x, norm_g, wv):
    s, d = x.shape
    width = wv.shape[1]
    n_col_steps = width // V_COLS
    kern = functools.partial(_vproj_kernel, n_col_steps=n_col_steps, width=width)
    return pl.pallas_call(
        kern,
        grid=(s // ROW_TILE, n_col_steps),
        in_specs=[
            pl.BlockSpec((ROW_TILE, d), lambda i, j: (i, 0)),
            pl.BlockSpec((1, d), lambda i, j: (0, 0)),
            pl.BlockSpec((d, V_COLS), lambda i, j: (0, j)),
        ],
        out_specs=[
            pl.BlockSpec((ROW_TILE, d), lambda i, j: (i, 0)),
            pl.BlockSpec((ROW_TILE, V_COLS), lambda i, j: (i, j)),
            pl.BlockSpec((ROW_TILE, 128), lambda i, j: (i, 0)),
            pl.BlockSpec((ROW_TILE, 128), lambda i, j: (i, 0)),
        ],
        out_shape=[
            jax.ShapeDtypeStruct((s, d), BF16),
            jax.ShapeDtypeStruct((s, width), BF16),
            jax.ShapeDtypeStruct((s, 128), F32),
            jax.ShapeDtypeStruct((s, 128), F32),
        ],
        scratch_shapes=[pltpu.VMEM((ROW_TILE, 1), F32), pltpu.VMEM((ROW_TILE, 1), F32)],
        compiler_params=pltpu.CompilerParams(
            dimension_semantics=("arbitrary", "arbitrary"), vmem_limit_bytes=VMEM_LIMIT),
        name="vproj",
    )(x, norm_g, wv)


def _linear_scan(a, b):
    n = a.shape[0]
    row = lax.broadcasted_iota(jnp.int32, a.shape, 0)
    step = 1
    while step < n:
        keep = row >= step
        a_prev = jnp.where(keep, pltpu.roll(a, step, 0), 1.0)
        b_prev = jnp.where(keep, pltpu.roll(b, step, 0), 0.0)
        b = a * b_prev + b
        a = a * a_prev
        step *= 2
    return a, b


def _heads_kernel(hn_ref, w_ref, gv_ref, mu_ref, rstd_ref, convw_ref, convb_ref,
                  wa_ref, ba_ref, wx_ref, bx_ref, lam_ref, lng_ref, lnb_ref, wsp_ref, bsp_ref,
                  mix_ref, hcar_ref, tail_ref):
    i = pl.program_id(0)
    j = pl.program_id(1)
    rows = hn_ref.shape[0]

    @pl.when(i == 0)
    def _():
        hcar_ref[j] = jnp.zeros((SUBLANES, BW), F32)
        tail_ref[j] = jnp.zeros((SUBLANES, BW), F32)

    hn = hn_ref[...]

    xa = _dot(hn, w_ref[0, :, 0:BW])
    ext = jnp.concatenate([tail_ref[j], xa], axis=0)
    tail_ref[j] = xa[rows - SUBLANES:, :]
    cw = convw_ref[...]
    xc = convb_ref[...] + xa * cw[CONV_W - 1:CONV_W, :]
    for lag in range(1, CONV_W):
        shifted = pltpu.roll(ext, lag, 0)[SUBLANES:, :]
        xc = xc + shifted * cw[CONV_W - 1 - lag:CONV_W - lag, :]
    xcb = xc.astype(BF16)
    r = _sigmoid(_dot(xcb, wa_ref[0]) + ba_ref[0])
    ig = _sigmoid(_dot(xcb, wx_ref[0]) + bx_ref[0])
    z = -lam_ref[...]
    softplus = jnp.maximum(z, 0.0) + jnp.log1p(jnp.exp(-jnp.abs(z)))
    log_a = (-LRU_C * softplus) * r
    a = jnp.exp(log_a)
    gain = jnp.sqrt(jnp.tanh(-log_a) * (1.0 + a * a))
    a_cum, h_loc = _linear_scan(a, gain * (ig * xc))
    h = h_loc + a_cum * hcar_ref[j][0:1, :]
    hcar_ref[j] = jnp.broadcast_to(h[rows - 1:rows, :], (SUBLANES, BW))
    ga = _dot(hn, w_ref[0, :, BW:2 * BW])
    mix_ref[:, 0:BW] = (h * (ga * _sigmoid(ga))).astype(BF16)

    gu = _gelu(_dot(hn, w_ref[0, :, 2 * BW:3 * BW]))
    gb = _dot(hn, w_ref[0, :, 3 * BW:4 * BW])
    sgb = gb * _sigmoid(gb)
    vn = (gv_ref[...].astype(F32) - mu_ref[:, 0:1]) * rstd_ref[:, 0:1] * lng_ref[...] + lnb_ref[...]
    vnb = vn.astype(BF16)
    blk_i = lax.broadcasted_iota(jnp.int32, (SPATIAL, SPATIAL), 0) // CHUNK
    blk_j = lax.broadcasted_iota(jnp.int32, (SPATIAL, SPATIAL), 1) // CHUNK
    ws = jnp.where(blk_j <= blk_i, wsp_ref[0], 0.0).astype(BF16)
    bsp = bsp_ref[0]
    for c in range(rows // SPATIAL):
        sl = slice(c * SPATIAL, (c + 1) * SPATIAL)
        sp = _dot(ws, vnb[sl, :]) + bsp
        mix_ref[sl, BW:2 * BW] = ((gu[sl, :] * sp) * sgb[sl, :]).astype(BF16)


def _heads(hn, w_heads, gv, mu, rstd, conv_w, conv_b, wa, ba, wx, bx, lam, lng, lnb, wsp, bsp):
    s, d = hn.shape
    per_head = lambda i, j: (0, j)
    head3 = lambda i, j: (j, 0, 0)
    return pl.pallas_call(
        _heads_kernel,
        grid=(s // ROW_TILE, H),
        in_specs=[
            pl.BlockSpec((ROW_TILE, d), lambda i, j: (i, 0)),
            pl.BlockSpec((1, d, 4 * BW), head3),
            pl.BlockSpec((ROW_TILE, BW), lambda i, j: (i, j)),
            pl.BlockSpec((ROW_TILE, 128), lambda i, j: (i, 0)),
            pl.BlockSpec((ROW_TILE, 128), lambda i, j: (i, 0)),
            pl.BlockSpec((CONV_W, BW), per_head),
            pl.BlockSpec((1, BW), per_head),
            pl.BlockSpec((1, BW, BW), head3),
            pl.BlockSpec((1, 1, BW), head3),
            pl.BlockSpec((1, BW, BW), head3),
            pl.BlockSpec((1, 1, BW), head3),
            pl.BlockSpec((1, BW), per_head),
            pl.BlockSpec((1, BW), per_head),
            pl.BlockSpec((1, BW), per_head),
            pl.BlockSpec((1, SPATIAL, SPATIAL), head3),
            pl.BlockSpec((1, SPATIAL, 1), head3),
        ],
        out_specs=pl.BlockSpec((ROW_TILE, 2 * BW), lambda i, j: (i, j)),
        out_shape=jax.ShapeDtypeStruct((s, 2 * BW * H), BF16),
        scratch_shapes=[pltpu.VMEM((H, SUBLANES, BW), F32), pltpu.VMEM((H, SUBLANES, BW), F32)],
        compiler_params=pltpu.CompilerParams(
            dimension_semantics=("arbitrary", "arbitrary"), vmem_limit_bytes=VMEM_LIMIT),
        name="heads",
    )(hn, w_heads, gv, mu, rstd, conv_w, conv_b, wa, ba, wx, bx, lam, lng, lnb, wsp, bsp)


def _out_kernel(mix_ref, wo_ref, x_ref, fg_ref, o_ref, *, n_k_steps, final_norm):
    k = pl.program_id(1)

    @pl.when(k == 0)
    def _():
        o_ref[...] = x_ref[...]

    o_ref[...] += _dot(mix_ref[...], wo_ref[...])

    if final_norm:
        @pl.when(k == n_k_steps - 1)
        def _():
            y = o_ref[...]
            ms = jnp.mean(y * y, axis=-1, keepdims=True)
            o_ref[...] = y * lax.rsqrt(ms + EPS) * fg_ref[...]


def _out_proj(mix, wo, x, final_g, final_norm):
    s, d = x.shape
    n_k_steps = mix.shape[1] // OUT_K
    kern = functools.partial(_out_kernel, n_k_steps=n_k_steps, final_norm=final_norm)
    return pl.pallas_call(
        kern,
        grid=(s // ROW_TILE, n_k_steps),
        in_specs=[
            pl.BlockSpec((ROW_TILE, OUT_K), lambda i, k: (i, k)),
            pl.BlockSpec((OUT_K, d), lambda i, k: (k, 0)),
            pl.BlockSpec((ROW_TILE, d), lambda i, k: (i, 0)),
            pl.BlockSpec((1, d), lambda i, k: (0, 0)),
        ],
        out_specs=pl.BlockSpec((ROW_TILE, d), lambda i, k: (i, 0)),
        out_shape=jax.ShapeDtypeStruct((s, d), F32),
        compiler_params=pltpu.CompilerParams(
            dimension_semantics=("arbitrary", "arbitrary"), vmem_limit_bytes=VMEM_LIMIT),
        name="out_proj",
    )(mix, wo, x, final_g)


def kernel(x, norm_g, w_in, conv_w, conv_b, w_gate_a, b_gate_a, w_gate_x, b_gate_x, lru_lambda,
           ln_v_g, ln_v_b, w_spatial, b_spatial, w_out, final_g):
    batch, seq, d = x.shape
    depth = w_in.shape[0]
    width = H * BW
    assert w_in.shape == (depth, d, 5 * width) and w_out.shape == (depth, 2 * width, d)
    assert (batch * seq) % ROW_TILE == 0 and seq % ROW_TILE == 0 and ROW_TILE % SPATIAL == 0
    outs = []
    for b in range(batch):
        xb = x[b]
        for l in range(depth):
            w5 = w_in[l].astype(BF16).reshape(d, 5, H, BW)
            w_heads = jnp.transpose(w5[:, jnp.array([0, 1, 2, 4])], (2, 0, 1, 3)).reshape(H, d, 4 * BW)
            wv = w5[:, 3].reshape(d, width)
            wo = jnp.transpose(w_out[l].astype(BF16).reshape(2, H, BW, d), (1, 0, 2, 3)).reshape(2 * width, d)

            hn, gv, mu, rstd = _vproj(xb, norm_g[l][None, :], wv)
            mix = _heads(
                hn, w_heads, gv, mu, rstd, conv_w[l], conv_b[l][None, :],
                w_gate_a[l].astype(BF16), b_gate_a[l][:, None, :],
                w_gate_x[l].astype(BF16), b_gate_x[l][:, None, :],
                lru_lambda[l][None, :], ln_v_g[l][None, :], ln_v_b[l][None, :],
                w_spatial[l], b_spatial[l][:, :, None])
            xb = _out_proj(mix, wo, xb, final_g[None, :], final_norm=(l == depth - 1))
        outs.append(xb)
    return jnp.stack(outs, axis=0)
```

```python
import functools
import math

import jax
import jax.numpy as jnp
from jax import lax
from jax.experimental import pallas as pl
from jax.experimental.pallas import tpu as pltpu

H = 16
BW = 256
CONV_W = 4
LRU_C = 8.0
SPATIAL = 128
CHUNK = 64
EPS = 1e-6

LANES = 128
SUBLANES = 8
ROW_TILE = 512
ROW_CHUNK = 256
V_COLS = 512
OUT_K = 512
SEG = ROW_CHUNK // SUBLANES
SEG_PITCH = SEG + SUBLANES
VMEM_LIMIT = 56 * 1024 * 1024

F32 = jnp.float32
BF16 = jnp.bfloat16


def _gelu(x):
    return 0.5 * x * (1.0 + lax.erf(x * math.sqrt(0.5)))


def _sigmoid(x):
    return 1.0 / (1.0 + jnp.exp(-x))


def _dot(a, b):
    return jnp.dot(a, b, preferred_element_type=F32)


def _vproj_kernel(x_ref, g_ref, wv_ref, hn_ref, gv_ref, mu_ref, rstd_ref, s1_ref, s2_ref,
                  *, n_col_steps, width):
    j = pl.program_id(1)

    @pl.when(j == 0)
    def _():
        x = x_ref[...]
        ms = jnp.mean(x * x, axis=-1, keepdims=True)
        hn_ref[...] = (x * lax.rsqrt(ms + EPS) * g_ref[...]).astype(BF16)
        s1_ref[...] = jnp.zeros_like(s1_ref)
        s2_ref[...] = jnp.zeros_like(s2_ref)

    gv = _gelu(_dot(hn_ref[...], wv_ref[...]))
    gv_ref[...] = gv.astype(BF16)
    s1_ref[...] += jnp.sum(gv, axis=-1, keepdims=True)
    s2_ref[...] += jnp.sum(gv * gv, axis=-1, keepdims=True)

    @pl.when(j == n_col_steps - 1)
    def _():
        mu = s1_ref[...] * (1.0 / width)
        var = jnp.maximum(s2_ref[...] * (1.0 / width) - mu * mu, 0.0)
        mu_ref[...] = jnp.broadcast_to(mu, mu_ref.shape)
        rstd_ref[...] = jnp.broadcast_to(lax.rsqrt(var + EPS), rstd_ref.shape)


def _vproj(x, norm_g, wv):
    s, d = x.shape
    width = wv.shape[1]
    n_col_steps = width // V_COLS
    kern = functools.partial(_vproj_kernel, n_col_steps=n_col_steps, width=width)
    return pl.pallas_call(
        kern,
        grid=(s // ROW_TILE, n_col_steps),
        in_specs=[
            pl.BlockSpec((ROW_TILE, d), lambda i, j: (i, 0)),
            pl.BlockSpec((1, d), lambda i, j: (0, 0)),
            pl.BlockSpec((d, V_COLS), lambda i, j: (0, j)),
        ],
        out_specs=[
            pl.BlockSpec((ROW_TILE, d), lambda i, j: (i, 0)),
            pl.BlockSpec((ROW_TILE, V_COLS), lambda i, j: (i, j)),
            pl.BlockSpec((ROW_TILE, LANES), lambda i, j: (i, 0)),
            pl.BlockSpec((ROW_TILE, LANES), lambda i, j: (i, 0)),
        ],
        out_shape=[
            jax.ShapeDtypeStruct((s, d), BF16),
            jax.ShapeDtypeStruct((s, width), BF16),
            jax.ShapeDtypeStruct((s, LANES), F32),
            jax.ShapeDtypeStruct((s, LANES), F32),
        ],
        scratch_shapes=[pltpu.VMEM((ROW_TILE, 1), F32), pltpu.VMEM((ROW_TILE, 1), F32)],
        compiler_params=pltpu.CompilerParams(
            dimension_semantics=("arbitrary", "arbitrary"), vmem_limit_bytes=VMEM_LIMIT),
        name="vproj",
    )(x, norm_g, wv)


def _scan_chunk(a, b, carry, a_s, b_s):
    n_cols = BW // LANES
    for s in range(SUBLANES):
        for c in range(n_cols):
            dst = pl.ds(s * SEG_PITCH, SEG)
            a_s[c, dst, :] = a[s * SEG:(s + 1) * SEG, c * LANES:(c + 1) * LANES]
            b_s[c, dst, :] = b[s * SEG:(s + 1) * SEG, c * LANES:(c + 1) * LANES]
    for c in range(n_cols):
        a_run = h_run = None
        for k in range(SEG):
            rows = pl.ds(k, SUBLANES, stride=SEG_PITCH)
            a_k = a_s[c, rows, :]
            b_k = b_s[c, rows, :]
            if k == 0:
                a_run, h_run = a_k, b_k
            else:
                h_run = a_k * h_run + b_k
                a_run = a_run * a_k
                a_s[c, rows, :] = a_run
                b_s[c, rows, :] = h_run
    out = []
    for s in range(SUBLANES):
        src = pl.ds(s * SEG_PITCH, SEG)
        a_seg = jnp.concatenate([a_s[c, src, :] for c in range(n_cols)], axis=1)
        h_seg = jnp.concatenate([b_s[c, src, :] for c in range(n_cols)], axis=1)
        h_seg = h_seg + a_seg * carry
        carry = h_seg[SEG - 1:SEG, :]
        out.append(h_seg)
    return jnp.concatenate(out, axis=0), carry


def _heads_kernel(hn_ref, w_ref, gv_ref, mu_ref, rstd_ref, convw_ref, convb_ref,
                  wa_ref, ba_ref, wx_ref, bx_ref, lam_ref, lng_ref, lnb_ref, wsp_ref, bsp_ref,
                  mix_ref, hcar_ref, tail_ref, a_s, b_s):
    i = pl.program_id(0)
    j = pl.program_id(1)
    rows = hn_ref.shape[0]

    @pl.when(i == 0)
    def _():
        hcar_ref[j] = jnp.zeros((SUBLANES, BW), F32)
        tail_ref[j] = jnp.zeros((SUBLANES, BW), F32)

    cw = convw_ref[...]
    z = -lam_ref[...]
    softplus = jnp.maximum(z, 0.0) + jnp.log1p(jnp.exp(-jnp.abs(z)))
    log_a_scale = -LRU_C * softplus
    blk_i = lax.broadcasted_iota(jnp.int32, (SPATIAL, SPATIAL), 0) // CHUNK
    blk_j = lax.broadcasted_iota(jnp.int32, (SPATIAL, SPATIAL), 1) // CHUNK
    ws = jnp.where(blk_j <= blk_i, wsp_ref[0], 0.0).astype(BF16)
    bsp = bsp_ref[0]

    tail = tail_ref[j]
    carry = hcar_ref[j][0:1, :]
    for c in range(rows // ROW_CHUNK):
        rs = slice(c * ROW_CHUNK, (c + 1) * ROW_CHUNK)
        p = _dot(hn_ref[rs, :], w_ref[0])
        xa, ga, u, gb = (p[:, k * BW:(k + 1) * BW] for k in range(4))

        ext = jnp.concatenate([tail, xa], axis=0)
        tail = xa[ROW_CHUNK - SUBLANES:, :]
        xc = convb_ref[...] + xa * cw[CONV_W - 1:CONV_W, :]
        for lag in range(1, CONV_W):
            shifted = pltpu.roll(ext, lag, 0)[SUBLANES:, :]
            xc = xc + shifted * cw[CONV_W - 1 - lag:CONV_W - lag, :]
        xcb = xc.astype(BF16)
        r = _sigmoid(_dot(xcb, wa_ref[0]) + ba_ref[0])
        ig = _sigmoid(_dot(xcb, wx_ref[0]) + bx_ref[0])
        log_a = log_a_scale * r
        a = jnp.exp(log_a)
        gain = jnp.sqrt(jnp.tanh(-log_a) * (1.0 + a * a))
        h, carry = _scan_chunk(a, gain * (ig * xc), carry, a_s, b_s)
        mix_ref[0, rs, :] = (h * (ga * _sigmoid(ga))).astype(BF16)

        gu = _gelu(u)
        sgb = gb * _sigmoid(gb)
        vn = ((gv_ref[rs, :].astype(F32) - mu_ref[rs, 0:1]) * rstd_ref[rs, 0:1] * lng_ref[...]
              + lnb_ref[...])
        vnb = vn.astype(BF16)
        for q in range(ROW_CHUNK // SPATIAL):
            sl = slice(q * SPATIAL, (q + 1) * SPATIAL)
            sp = _dot(ws, vnb[sl, :]) + bsp
            out_rows = slice(c * ROW_CHUNK + q * SPATIAL, c * ROW_CHUNK + (q + 1) * SPATIAL)
            mix_ref[1, out_rows, :] = ((gu[sl, :] * sp) * sgb[sl, :]).astype(BF16)

    tail_ref[j] = tail
    hcar_ref[j] = jnp.broadcast_to(carry, (SUBLANES, BW))


def _heads(hn, w_heads, gv, mu, rstd, conv_w, conv_b, wa, ba, wx, bx, lam, lng, lnb, wsp, bsp):
    s, d = hn.shape
    per_head = lambda i, j: (0, j)
    head3 = lambda i, j: (j, 0, 0)
    return pl.pallas_call(
        _heads_kernel,
        grid=(s // ROW_TILE, H),
        in_specs=[
            pl.BlockSpec((ROW_TILE, d), lambda i, j: (i, 0)),
            pl.BlockSpec((1, d, 4 * BW), head3),
            pl.BlockSpec((ROW_TILE, BW), lambda i, j: (i, j)),
            pl.BlockSpec((ROW_TILE, LANES), lambda i, j: (i, 0)),
            pl.BlockSpec((ROW_TILE, LANES), lambda i, j: (i, 0)),
            pl.BlockSpec((CONV_W, BW), per_head),
            pl.BlockSpec((1, BW), per_head),
            pl.BlockSpec((1, BW, BW), head3),
            pl.BlockSpec((1, 1, BW), head3),
            pl.BlockSpec((1, BW, BW), head3),
            pl.BlockSpec((1, 1, BW), head3),
            pl.BlockSpec((1, BW), per_head),
            pl.BlockSpec((1, BW), per_head),
            pl.BlockSpec((1, BW), per_head),
            pl.BlockSpec((1, SPATIAL, SPATIAL), head3),
            pl.BlockSpec((1, SPATIAL, 1), head3),
        ],
        out_specs=pl.BlockSpec((2, ROW_TILE, BW), lambda i, j: (0, i, j)),
        out_shape=jax.ShapeDtypeStruct((2, s, BW * H), BF16),
        scratch_shapes=[
            pltpu.VMEM((H, SUBLANES, BW), F32),
            pltpu.VMEM((H, SUBLANES, BW), F32),
            pltpu.VMEM((BW // LANES, SUBLANES * SEG_PITCH, LANES), F32),
            pltpu.VMEM((BW // LANES, SUBLANES * SEG_PITCH, LANES), F32),
        ],
        compiler_params=pltpu.CompilerParams(
            dimension_semantics=("arbitrary", "arbitrary"), vmem_limit_bytes=VMEM_LIMIT),
        name="heads",
    )(hn, w_heads, gv, mu, rstd, conv_w, conv_b, wa, ba, wx, bx, lam, lng, lnb, wsp, bsp)


def _out_kernel(mix_ref, wo_ref, x_ref, fg_ref, o_ref, *, n_k_steps, final_norm):
    k = pl.program_id(1)

    @pl.when(k == 0)
    def _():
        o_ref[...] = x_ref[...]

    o_ref[...] += _dot(mix_ref[0], wo_ref[...])

    if final_norm:
        @pl.when(k == n_k_steps - 1)
        def _():
            y = o_ref[...]
            ms = jnp.mean(y * y, axis=-1, keepdims=True)
            o_ref[...] = y * lax.rsqrt(ms + EPS) * fg_ref[...]


def _out_proj(mix, wo, x, final_g, final_norm):
    s, d = x.shape
    n_half = mix.shape[2] // OUT_K
    n_k_steps = mix.shape[0] * n_half
    kern = functools.partial(_out_kernel, n_k_steps=n_k_steps, final_norm=final_norm)
    return pl.pallas_call(
        kern,
        grid=(s // ROW_TILE, n_k_steps),
        in_specs=[
            pl.BlockSpec((1, ROW_TILE, OUT_K), lambda i, k: (k // n_half, i, k % n_half)),
            pl.BlockSpec((OUT_K, d), lambda i, k: (k, 0)),
            pl.BlockSpec((ROW_TILE, d), lambda i, k: (i, 0)),
            pl.BlockSpec((1, d), lambda i, k: (0, 0)),
        ],
        out_specs=pl.BlockSpec((ROW_TILE, d), lambda i, k: (i, 0)),
        out_shape=jax.ShapeDtypeStruct((s, d), F32),
        compiler_params=pltpu.CompilerParams(
            dimension_semantics=("arbitrary", "arbitrary"), vmem_limit_bytes=VMEM_LIMIT),
        name="out_proj",
    )(mix, wo, x, final_g)


def kernel(x, norm_g, w_in, conv_w, conv_b, w_gate_a, b_gate_a, w_gate_x, b_gate_x, lru_lambda,
           ln_v_g, ln_v_b, w_spatial, b_spatial, w_out, final_g):
    batch, seq, d = x.shape
    depth = w_in.shape[0]
    width = H * BW
    assert w_in.shape == (depth, d, 5 * width) and w_out.shape == (depth, 2 * width, d)
    assert seq % ROW_TILE == 0 and ROW_TILE % ROW_CHUNK == 0 and ROW_CHUNK % SPATIAL == 0
    outs = []
    for b in range(batch):
        xb = x[b]
        for l in range(depth):
            w5 = w_in[l].reshape(d, 5, H, BW)
            w4 = jnp.concatenate([w5[:, 0:3], w5[:, 4:5]], axis=1)
            w_heads = jnp.transpose(w4, (2, 0, 1, 3)).astype(BF16).reshape(H, d, 4 * BW)
            wv = w_in[l][:, 3 * width:4 * width].astype(BF16)
            wo = w_out[l].astype(BF16)

            hn, gv, mu, rstd = _vproj(xb, norm_g[l][None, :], wv)
            mix = _heads(
                hn, w_heads, gv, mu, rstd, conv_w[l], conv_b[l][None, :],
                w_gate_a[l].astype(BF16), b_gate_a[l][:, None, :],
                w_gate_x[l].astype(BF16), b_gate_x[l][:, None, :],
                lru_lambda[l][None, :], ln_v_g[l][None, :], ln_v_b[l][None, :],
                w_spatial[l], b_spatial[l][:, :, None])
            xb = _out_proj(mix, wo, xb, final_g[None, :], final_norm=(l == depth - 1))
        outs.append(xb)
    return jnp.stack(outs, axis=0)
```

```python
import functools
import math

import jax
import jax.numpy as jnp
from jax import lax
from jax.experimental import pallas as pl
from jax.experimental.pallas import tpu as pltpu

H = 16
BW = 256
CONV_W = 4
LRU_C = 8.0
SPATIAL = 128
CHUNK = 64
EPS = 1e-6

LANES = 128
SUBLANES = 8
ROW_TILE = 512
ROW_CHUNK = 256
V_COLS = 512
OUT_N = 512
REGROUP_ROWS = 1024
SEG = ROW_CHUNK // SUBLANES
SEG_PITCH = SEG + SUBLANES
VMEM_LIMIT = 56 * 1024 * 1024

F32 = jnp.float32
BF16 = jnp.bfloat16


def _gelu(x):
    return 0.5 * x * (1.0 + lax.erf(x * math.sqrt(0.5)))


def _sigmoid(x):
    return 1.0 / (1.0 + jnp.exp(-x))


def _dot(a, b):
    return jnp.dot(a, b, preferred_element_type=F32)


def _regroup_kernel(xa_ref, ga_ref, u_ref, v_ref, gb_ref, wh_ref, wv_ref):
    for k, ref in enumerate((xa_ref, ga_ref, u_ref, gb_ref)):
        wh_ref[0, :, k * BW:(k + 1) * BW] = ref[...].astype(BF16)
    wv_ref[...] = v_ref[...].astype(BF16)


def _regroup(w_in):
    d = w_in.shape[0]
    group = lambda g: (lambda j, k: (k, g * H + j))
    return pl.pallas_call(
        _regroup_kernel,
        grid=(H, d // REGROUP_ROWS),
        in_specs=[pl.BlockSpec((REGROUP_ROWS, BW), group(g)) for g in range(5)],
        out_specs=[
            pl.BlockSpec((1, REGROUP_ROWS, 4 * BW), lambda j, k: (j, k, 0)),
            pl.BlockSpec((REGROUP_ROWS, BW), lambda j, k: (k, j)),
        ],
        out_shape=[
            jax.ShapeDtypeStruct((H, d, 4 * BW), BF16),
            jax.ShapeDtypeStruct((d, H * BW), BF16),
        ],
        compiler_params=pltpu.CompilerParams(dimension_semantics=("arbitrary", "arbitrary")),
        name="regroup",
    )(w_in, w_in, w_in, w_in, w_in)


def _vproj_kernel(x_ref, g_ref, wv_ref, hn_ref, gv_ref, mu_ref, rstd_ref, s1_ref, s2_ref,
                  *, n_col_steps, width):
    j = pl.program_id(1)

    @pl.when(j == 0)
    def _():
        x = x_ref[...]
        ms = jnp.mean(x * x, axis=-1, keepdims=True)
        hn_ref[...] = (x * lax.rsqrt(ms + EPS) * g_ref[...]).astype(BF16)
        s1_ref[...] = jnp.zeros_like(s1_ref)
        s2_ref[...] = jnp.zeros_like(s2_ref)

    gv = _gelu(_dot(hn_ref[...], wv_ref[...]))
    gv_ref[...] = gv.astype(BF16)
    s1_ref[...] += jnp.sum(gv, axis=-1, keepdims=True)
    s2_ref[...] += jnp.sum(gv * gv, axis=-1, keepdims=True)

    @pl.when(j == n_col_steps - 1)
    def _():
        mu = s1_ref[...] * (1.0 / width)
        var = jnp.maximum(s2_ref[...] * (1.0 / width) - mu * mu, 0.0)
        mu_ref[...] = jnp.broadcast_to(mu, mu_ref.shape)
        rstd_ref[...] = jnp.broadcast_to(lax.rsqrt(var + EPS), rstd_ref.shape)


def _vproj(x, norm_g, wv):
    s, d = x.shape
    width = wv.shape[1]
    n_col_steps = width // V_COLS
    kern = functools.partial(_vproj_kernel, n_col_steps=n_col_steps, width=width)
    return pl.pallas_call(
        kern,
        grid=(s // ROW_TILE, n_col_steps),
        in_specs=[
            pl.BlockSpec((ROW_TILE, d), lambda i, j: (i, 0)),
            pl.BlockSpec((1, d), lambda i, j: (0, 0)),
            pl.BlockSpec((d, V_COLS), lambda i, j: (0, j)),
        ],
        out_specs=[
            pl.BlockSpec((ROW_TILE, d), lambda i, j: (i, 0)),
            pl.BlockSpec((ROW_TILE, V_COLS), lambda i, j: (i, j)),
            pl.BlockSpec((ROW_TILE, LANES), lambda i, j: (i, 0)),
            pl.BlockSpec((ROW_TILE, LANES), lambda i, j: (i, 0)),
        ],
        out_shape=[
            jax.ShapeDtypeStruct((s, d), BF16),
            jax.ShapeDtypeStruct((s, width), BF16),
            jax.ShapeDtypeStruct((s, LANES), F32),
            jax.ShapeDtypeStruct((s, LANES), F32),
        ],
        scratch_shapes=[pltpu.VMEM((ROW_TILE, 1), F32), pltpu.VMEM((ROW_TILE, 1), F32)],
        compiler_params=pltpu.CompilerParams(
            dimension_semantics=("arbitrary", "arbitrary"), vmem_limit_bytes=VMEM_LIMIT),
        name="vproj",
    )(x, norm_g, wv)


def _scan_chunk(a, b, carry, a_s, b_s):
    n_cols = BW // LANES
    for s in range(SUBLANES):
        for c in range(n_cols):
            dst = pl.ds(s * SEG_PITCH, SEG)
            a_s[c, dst, :] = a[s * SEG:(s + 1) * SEG, c * LANES:(c + 1) * LANES]
            b_s[c, dst, :] = b[s * SEG:(s + 1) * SEG, c * LANES:(c + 1) * LANES]
    for c in range(n_cols):
        a_run = h_run = None
        for k in range(SEG):
            rows = pl.ds(k, SUBLANES, stride=SEG_PITCH)
            a_k = a_s[c, rows, :]
            b_k = b_s[c, rows, :]
            if k == 0:
                a_run, h_run = a_k, b_k
            else:
                h_run = a_k * h_run + b_k
                a_run = a_run * a_k
                a_s[c, rows, :] = a_run
                b_s[c, rows, :] = h_run
    out = []
    for s in range(SUBLANES):
        src = pl.ds(s * SEG_PITCH, SEG)
        a_seg = jnp.concatenate([a_s[c, src, :] for c in range(n_cols)], axis=1)
        h_seg = jnp.concatenate([b_s[c, src, :] for c in range(n_cols)], axis=1)
        h_seg = h_seg + a_seg * carry
        carry = h_seg[SEG - 1:SEG, :]
        out.append(h_seg)
    return jnp.concatenate(out, axis=0), carry


def _heads_kernel(hn_ref, w_ref, gv_ref, mu_ref, rstd_ref, convw_ref, convb_ref,
                  wa_ref, ba_ref, wx_ref, bx_ref, lam_ref, lng_ref, lnb_ref, wsp_ref, bsp_ref,
                  mix_ref, hcar_ref, tail_ref, a_s, b_s):
    i = pl.program_id(0)
    j = pl.program_id(1)
    rows = hn_ref.shape[0]

    @pl.when(i == 0)
    def _():
        hcar_ref[j] = jnp.zeros((SUBLANES, BW), F32)
        tail_ref[j] = jnp.zeros((SUBLANES, BW), F32)

    cw = convw_ref[...]
    z = -lam_ref[...]
    softplus = jnp.maximum(z, 0.0) + jnp.log1p(jnp.exp(-jnp.abs(z)))
    log_a_scale = -LRU_C * softplus
    blk_i = lax.broadcasted_iota(jnp.int32, (SPATIAL, SPATIAL), 0) // CHUNK
    blk_j = lax.broadcasted_iota(jnp.int32, (SPATIAL, SPATIAL), 1) // CHUNK
    ws = jnp.where(blk_j <= blk_i, wsp_ref[0], 0.0).astype(BF16)
    bsp = bsp_ref[0]

    tail = tail_ref[j]
    carry = hcar_ref[j][0:1, :]
    for c in range(rows // ROW_CHUNK):
        rs = slice(c * ROW_CHUNK, (c + 1) * ROW_CHUNK)
        p = _dot(hn_ref[rs, :], w_ref[0])
        xa, ga, u, gb = (p[:, k * BW:(k + 1) * BW] for k in range(4))

        ext = jnp.concatenate([tail, xa], axis=0)
        tail = xa[ROW_CHUNK - SUBLANES:, :]
        xc = convb_ref[...] + xa * cw[CONV_W - 1:CONV_W, :]
        for lag in range(1, CONV_W):
            shifted = pltpu.roll(ext, lag, 0)[SUBLANES:, :]
            xc = xc + shifted * cw[CONV_W - 1 - lag:CONV_W - lag, :]
        xcb = xc.astype(BF16)
        r = _sigmoid(_dot(xcb, wa_ref[0]) + ba_ref[0])
        ig = _sigmoid(_dot(xcb, wx_ref[0]) + bx_ref[0])
        log_a = log_a_scale * r
        a = jnp.exp(log_a)
        gain = jnp.sqrt(jnp.tanh(-log_a) * (1.0 + a * a))
        h, carry = _scan_chunk(a, gain * (ig * xc), carry, a_s, b_s)
        mix_ref[0, rs, :] = (h * (ga * _sigmoid(ga))).astype(BF16)

        gu = _gelu(u)
        sgb = gb * _sigmoid(gb)
        vn = ((gv_ref[rs, :].astype(F32) - mu_ref[rs, 0:1]) * rstd_ref[rs, 0:1] * lng_ref[...]
              + lnb_ref[...])
        vnb = vn.astype(BF16)
        for q in range(ROW_CHUNK // SPATIAL):
            sl = slice(q * SPATIAL, (q + 1) * SPATIAL)
            sp = _dot(ws, vnb[sl, :]) + bsp
            out_rows = slice(c * ROW_CHUNK + q * SPATIAL, c * ROW_CHUNK + (q + 1) * SPATIAL)
            mix_ref[1, out_rows, :] = ((gu[sl, :] * sp) * sgb[sl, :]).astype(BF16)

    tail_ref[j] = tail
    hcar_ref[j] = jnp.broadcast_to(carry, (SUBLANES, BW))


def _heads(hn, w_heads, gv, mu, rstd, conv_w, conv_b, wa, ba, wx, bx, lam, lng, lnb, wsp, bsp):
    s, d = hn.shape
    per_head = lambda i, j: (0, j)
    head3 = lambda i, j: (j, 0, 0)
    return pl.pallas_call(
        _heads_kernel,
        grid=(s // ROW_TILE, H),
        in_specs=[
            pl.BlockSpec((ROW_TILE, d), lambda i, j: (i, 0)),
            pl.BlockSpec((1, d, 4 * BW), head3),
            pl.BlockSpec((ROW_TILE, BW), lambda i, j: (i, j)),
            pl.BlockSpec((ROW_TILE, LANES), lambda i, j: (i, 0)),
            pl.BlockSpec((ROW_TILE, LANES), lambda i, j: (i, 0)),
            pl.BlockSpec((CONV_W, BW), per_head),
            pl.BlockSpec((1, BW), per_head),
            pl.BlockSpec((1, BW, BW), head3),
            pl.BlockSpec((1, 1, BW), head3),
            pl.BlockSpec((1, BW, BW), head3),
            pl.BlockSpec((1, 1, BW), head3),
            pl.BlockSpec((1, BW), per_head),
            pl.BlockSpec((1, BW), per_head),
            pl.BlockSpec((1, BW), per_head),
            pl.BlockSpec((1, SPATIAL, SPATIAL), head3),
            pl.BlockSpec((1, SPATIAL, 1), head3),
        ],
        out_specs=pl.BlockSpec((2, ROW_TILE, BW), lambda i, j: (0, i, j)),
        out_shape=jax.ShapeDtypeStruct((2, s, BW * H), BF16),
        scratch_shapes=[
            pltpu.VMEM((H, SUBLANES, BW), F32),
            pltpu.VMEM((H, SUBLANES, BW), F32),
            pltpu.VMEM((BW // LANES, SUBLANES * SEG_PITCH, LANES), F32),
            pltpu.VMEM((BW // LANES, SUBLANES * SEG_PITCH, LANES), F32),
        ],
        compiler_params=pltpu.CompilerParams(
            dimension_semantics=("arbitrary", "arbitrary"), vmem_limit_bytes=VMEM_LIMIT),
        name="heads",
    )(hn, w_heads, gv, mu, rstd, conv_w, conv_b, wa, ba, wx, bx, lam, lng, lnb, wsp, bsp)


def _out_kernel(mix_ref, wo_ref, x_ref, fg_ref, o_ref, *, n_col_steps, final_norm):
    n = pl.program_id(1)
    y = x_ref[...] + _dot(mix_ref[0], wo_ref[0]) + _dot(mix_ref[1], wo_ref[1])
    for nn in range(n_col_steps):
        @pl.when(n == nn)
        def _():
            o_ref[:, nn * OUT_N:(nn + 1) * OUT_N] = y

    if final_norm:
        @pl.when(n == n_col_steps - 1)
        def _():
            z = o_ref[...]
            ms = jnp.mean(z * z, axis=-1, keepdims=True)
            o_ref[...] = z * lax.rsqrt(ms + EPS) * fg_ref[...]


def _out_proj(mix, wo, x, final_g, final_norm):
    s, d = x.shape
    halves, _, width = mix.shape
    n_col_steps = d // OUT_N
    kern = functools.partial(_out_kernel, n_col_steps=n_col_steps, final_norm=final_norm)
    return pl.pallas_call(
        kern,
        grid=(s // ROW_TILE, n_col_steps),
        in_specs=[
            pl.BlockSpec((halves, ROW_TILE, width), lambda i, n: (0, i, 0),
                         pipeline_mode=pl.Buffered(1)),
            pl.BlockSpec((halves, width, OUT_N), lambda i, n: (0, 0, n)),
            pl.BlockSpec((ROW_TILE, OUT_N), lambda i, n: (i, n)),
            pl.BlockSpec((1, d), lambda i, n: (0, 0)),
        ],
        out_specs=pl.BlockSpec((ROW_TILE, d), lambda i, n: (i, 0)),
        out_shape=jax.ShapeDtypeStruct((s, d), F32),
        compiler_params=pltpu.CompilerParams(
            dimension_semantics=("arbitrary", "arbitrary"), vmem_limit_bytes=VMEM_LIMIT),
        name="out_proj",
    )(mix, wo.reshape(halves, width, d), x, final_g)


def kernel(x, norm_g, w_in, conv_w, conv_b, w_gate_a, b_gate_a, w_gate_x, b_gate_x, lru_lambda,
           ln_v_g, ln_v_b, w_spatial, b_spatial, w_out, final_g):
    batch, seq, d = x.shape
    depth = w_in.shape[0]
    width = H * BW
    assert w_in.shape == (depth, d, 5 * width) and w_out.shape == (depth, 2 * width, d)
    assert seq % ROW_TILE == 0 and ROW_TILE % ROW_CHUNK == 0 and ROW_CHUNK % SPATIAL == 0
    assert d % REGROUP_ROWS == 0
    outs = []
    for b in range(batch):
        xb = x[b]
        for l in range(depth):
            w_heads, wv = _regroup(w_in[l])
            wo = w_out[l].astype(BF16)
            hn, gv, mu, rstd = _vproj(xb, norm_g[l][None, :], wv)
            mix = _heads(
                hn, w_heads, gv, mu, rstd, conv_w[l], conv_b[l][None, :],
                w_gate_a[l].astype(BF16), b_gate_a[l][:, None, :],
                w_gate_x[l].astype(BF16), b_gate_x[l][:, None, :],
                lru_lambda[l][None, :], ln_v_g[l][None, :], ln_v_b[l][None, :],
                w_spatial[l], b_spatial[l][:, :, None])
            xb = _out_proj(mix, wo, xb, final_g[None, :], final_norm=(l == depth - 1))
        outs.append(xb)
    return jnp.stack(outs, axis=0)
```
